```python
import jax, jax.numpy as jnp
from jax import lax
import numpy as np

D_MODEL = 1024
BATCH = 8
SEQ = 4096
DEPTH = 2
DEC_BATCH = 32
DEC_SEQ = 2048
PAST_LEN = 128

GRID_W = 64
N_MIXERS = 2
N_ATTN_LAYERS = (DEPTH + N_MIXERS - 1) // N_MIXERS
N_SGU_LAYERS = DEPTH // N_MIXERS
HEAD_DIM = 128
N_HEADS = D_MODEL // HEAD_DIM
N_KV_HEADS = N_HEADS // 4
Q_PER_KV = N_HEADS // N_KV_HEADS
QKV_WIDTH = (N_HEADS + 2 * N_KV_HEADS) * HEAD_DIM
ROPE_PAIRS = HEAD_DIM // 4
ROPE_THETA = 10000.0
Q_BLOCK = 128
SGU_WIDTH = 2 * D_MODEL
SGU_GROUPS = 8
SGU_GROUP_DIM = SGU_WIDTH // SGU_GROUPS
CHUNK = 128
N_GROUPS = 4
EXPERTS_PER_GROUP = 8
N_EXPERTS = N_GROUPS * EXPERTS_PER_GROUP
EXPERT_FF = D_MODEL // 4
TOP_K_IN_GROUP = 2
EPS = 1e-6

kernel_name = 'hybrid_gqa_sgu_hmoe_encoder'


def rms_norm(x, g):
    xf = x.astype(jnp.float32)
    y = xf * lax.rsqrt(jnp.mean(xf * xf, axis=-1, keepdims=True) + EPS)
    return (y * g.astype(jnp.float32)).astype(x.dtype)


def axial_rope(S, dtype):
    rows = S // GRID_W
    row = jnp.repeat(jnp.arange(rows, dtype=jnp.float32), GRID_W)
    col = jnp.tile(jnp.arange(GRID_W, dtype=jnp.float32), rows)
    inv = ROPE_THETA ** (-jnp.arange(ROPE_PAIRS, dtype=jnp.float32) / ROPE_PAIRS)
    ang = jnp.stack([row[:, None] * inv, col[:, None] * inv], axis=1)
    return jnp.cos(ang)[:, None].astype(dtype), jnp.sin(ang)[:, None].astype(dtype)


def apply_rope(x, cos, sin):
    xr = x.reshape(x.shape[:-1] + (2, 2, ROPE_PAIRS))
    x1, x2 = xr[..., 0, :], xr[..., 1, :]
    out = jnp.stack([x1 * cos - x2 * sin, x2 * cos + x1 * sin], axis=-2)
    return out.reshape(x.shape)


def attention_mixer(h, w_in, q_gain, k_gain, w_out):
    B, S, _ = h.shape
    qkv = h @ w_in
    nq, nk = N_HEADS * HEAD_DIM, N_KV_HEADS * HEAD_DIM
    q = qkv[..., :nq].reshape(B, S, N_HEADS, HEAD_DIM)
    k = qkv[..., nq:nq + nk].reshape(B, S, N_KV_HEADS, HEAD_DIM)
    v = qkv[..., nq + nk:].reshape(B, S, N_KV_HEADS, HEAD_DIM)
    q = rms_norm(q, q_gain)
    k = rms_norm(k, k_gain)
    cos, sin = axial_rope(S, q.dtype)
    q = apply_rope(q, cos, sin)
    k = apply_rope(k, cos, sin)
    nb = S // Q_BLOCK
    qb = q.reshape(B, nb, Q_BLOCK, N_KV_HEADS, Q_PER_KV, HEAD_DIM).transpose(1, 0, 2, 3, 4, 5)
    scale = HEAD_DIM ** -0.5

    def block(q_blk):
        s = jnp.einsum('bqkgd,bskd->bkgqs', q_blk, k).astype(jnp.float32) * scale
        p = jax.nn.softmax(s, axis=-1).astype(v.dtype)
        return jnp.einsum('bkgqs,bskd->bqkgd', p, v)

    o = lax.map(block, qb)
    o = o.transpose(1, 0, 2, 3, 4, 5).reshape(B, S, nq)
    return o @ w_out


def sgu_mixer(h, w_in, v_gain, w_s, b_s, w_out):
    B, S, _ = h.shape
    z = jax.nn.gelu(h @ w_in, approximate=False)
    u, v = z[..., :SGU_WIDTH], z[..., SGU_WIDTH:]
    v = rms_norm(v, v_gain)
    nc = S // CHUNK
    vc = v.reshape(B, nc, CHUNK, SGU_GROUPS, SGU_GROUP_DIM)
    s = jnp.einsum('gpq,bcqgd->bcpgd', w_s, vc) + b_s.T[:, :, None]
    s = s.reshape(B, S, SGU_WIDTH)
    return (u * s) @ w_out


def hierarchical_moe(h, w_group, b_group, w_router, b_router, w_gate, w_up, w_down):
    B, S, D = h.shape
    t = h.reshape(-1, D)
    T = t.shape[0]
    g_prob = jax.nn.softmax((t @ w_group).astype(jnp.float32) + b_group.astype(jnp.float32), axis=-1)
    g_val, g_idx = lax.top_k(g_prob, 1)
    e_all = jnp.einsum('td,dge->tge', t, w_router).astype(jnp.float32) + b_router.astype(jnp.float32)
    e_logits = jnp.take_along_axis(e_all, g_idx[:, :, None], axis=1)[:, 0]
    e_prob = jax.nn.softmax(e_logits, axis=-1)
    e_val, e_idx = lax.top_k(e_prob, TOP_K_IN_GROUP)
    e_val = e_val / jnp.sum(e_val, axis=-1, keepdims=True)
    w_in_group = jnp.einsum('tk,tke->te', e_val,
                            jax.nn.one_hot(e_idx, EXPERTS_PER_GROUP, dtype=jnp.float32))
    combine = (jax.nn.one_hot(g_idx[:, 0], N_GROUPS, dtype=jnp.float32)[:, :, None]
               * (g_val * w_in_group)[:, None, :]).reshape(T, N_EXPERTS)

    def expert_step(acc, xs):
        wg, wu, wd, c = xs
        y = (jax.nn.silu(t @ wg) * (t @ wu)) @ wd
        return acc + c[:, None] * y.astype(jnp.float32), None

    acc, _ = lax.scan(expert_step, jnp.zeros((T, D), jnp.float32),
                      (w_gate, w_up, w_down, combine.T))
    return acc.astype(h.dtype).reshape(B, S, D)


def encoder(x, attn_norm, attn_w_in, attn_q_gain, attn_k_gain, attn_w_out,
            sgu_norm, sgu_w_in, sgu_v_gain, sgu_w_s, sgu_b_s, sgu_w_out,
            moe_norm, moe_w_group, moe_b_group, moe_w_router, moe_b_router,
            moe_w_gate, moe_w_up, moe_w_down):
    for i in range(DEPTH):
        j = i // N_MIXERS
        if i % N_MIXERS == 0:
            x = x + attention_mixer(rms_norm(x, attn_norm[j]), attn_w_in[j], attn_q_gain[j],
                                    attn_k_gain[j], attn_w_out[j])
        else:
            x = x + sgu_mixer(rms_norm(x, sgu_norm[j]), sgu_w_in[j], sgu_v_gain[j],
                              sgu_w_s[j], sgu_b_s[j], sgu_w_out[j])
        x = x + hierarchical_moe(rms_norm(x, moe_norm[i]), moe_w_group[i], moe_b_group[i],
                                 moe_w_router[i], moe_b_router[i], moe_w_gate[i],
                                 moe_w_up[i], moe_w_down[i])
    return x


def setup_inputs(seed: int = 0) -> dict:
    key = jax.random.key(seed)
    ks = jax.random.split(key, 24)
    f32 = jnp.float32

    def nrm(k, shape, scale):
        return jax.random.normal(k, shape, f32) * scale

    def gain(k, shape):
        return 1.0 + 0.1 * jax.random.normal(k, shape, f32)

    return {
        'x_prompt': nrm(ks[0], (BATCH, SEQ, D_MODEL), 1.0),
        'x_sample': nrm(ks[1], (DEC_BATCH, DEC_SEQ, D_MODEL), 1.0),
        'attn_norm': gain(ks[2], (N_ATTN_LAYERS, D_MODEL)),
        'attn_w_in': nrm(ks[3], (N_ATTN_LAYERS, D_MODEL, QKV_WIDTH), D_MODEL ** -0.5),
        'attn_q_gain': gain(ks[4], (N_ATTN_LAYERS, HEAD_DIM)),
        'attn_k_gain': gain(ks[5], (N_ATTN_LAYERS, HEAD_DIM)),
        'attn_w_out': nrm(ks[6], (N_ATTN_LAYERS, N_HEADS * HEAD_DIM, D_MODEL), (N_HEADS * HEAD_DIM) ** -0.5),
        'sgu_norm': gain(ks[7], (N_SGU_LAYERS, D_MODEL)),
        'sgu_w_in': nrm(ks[8], (N_SGU_LAYERS, D_MODEL, 2 * SGU_WIDTH), D_MODEL ** -0.5),
        'sgu_v_gain': gain(ks[9], (N_SGU_LAYERS, SGU_WIDTH)),
        'sgu_w_s': nrm(ks[10], (N_SGU_LAYERS, SGU_GROUPS, CHUNK, CHUNK), CHUNK ** -0.5),
        'sgu_b_s': 1.0 + 0.1 * jax.random.normal(ks[11], (N_SGU_LAYERS, SGU_GROUPS, CHUNK), f32),
        'sgu_w_out': nrm(ks[12], (N_SGU_LAYERS, SGU_WIDTH, D_MODEL), SGU_WIDTH ** -0.5),
        'moe_norm': gain(ks[13], (DEPTH, D_MODEL)),
        'moe_w_group': nrm(ks[14], (DEPTH, D_MODEL, N_GROUPS), D_MODEL ** -0.5),
        'moe_b_group': nrm(ks[15], (DEPTH, N_GROUPS), 0.01),
        'moe_w_router': nrm(ks[16], (DEPTH, D_MODEL, N_GROUPS, EXPERTS_PER_GROUP), D_MODEL ** -0.5),
        'moe_b_router': nrm(ks[17], (DEPTH, N_GROUPS, EXPERTS_PER_GROUP), 0.01),
        'moe_w_gate': nrm(ks[18], (DEPTH, N_EXPERTS, D_MODEL, EXPERT_FF), D_MODEL ** -0.5),
        'moe_w_up': nrm(ks[19], (DEPTH, N_EXPERTS, D_MODEL, EXPERT_FF), D_MODEL ** -0.5),
        'moe_w_down': nrm(ks[20], (DEPTH, N_EXPERTS, EXPERT_FF, D_MODEL), EXPERT_FF ** -0.5),
    }


def reference(x_prompt, x_sample, attn_norm, attn_w_in, attn_q_gain, attn_k_gain, attn_w_out,
              sgu_norm, sgu_w_in, sgu_v_gain, sgu_w_s, sgu_b_s, sgu_w_out,
              moe_norm, moe_w_group, moe_b_group, moe_w_router, moe_b_router,
              moe_w_gate, moe_w_up, moe_w_down):
    y_prompt = encoder(x_prompt, attn_norm, attn_w_in, attn_q_gain, attn_k_gain, attn_w_out,
                       sgu_norm, sgu_w_in, sgu_v_gain, sgu_w_s, sgu_b_s, sgu_w_out,
                       moe_norm, moe_w_group, moe_b_group, moe_w_router, moe_b_router,
                       moe_w_gate, moe_w_up, moe_w_down)
    y_sample = encoder(x_sample, attn_norm, attn_w_in, attn_q_gain, attn_k_gain, attn_w_out,
                       sgu_norm, sgu_w_in, sgu_v_gain, sgu_w_s, sgu_b_s, sgu_w_out,
                       moe_norm, moe_w_group, moe_b_group, moe_w_router, moe_b_router,
                       moe_w_gate, moe_w_up, moe_w_down)
    return (y_prompt, y_sample)
```

```python
import functools
import math

import numpy as np
import jax
import jax.numpy as jnp
from jax import lax
from jax.experimental import pallas as pl
from jax.experimental.pallas import tpu as pltpu

F32 = jnp.float32
BF16 = jnp.bfloat16

D_MODEL = 1024
GRID_W = 64
HEAD_DIM = 128
N_HEADS = 8
N_KV_HEADS = 2
Q_PER_KV = N_HEADS // N_KV_HEADS
Q_WIDTH = N_HEADS * HEAD_DIM
KV_WIDTH = N_KV_HEADS * HEAD_DIM
QKV_WIDTH = Q_WIDTH + 2 * KV_WIDTH
ROPE_PAIRS = HEAD_DIM // 4
ROPE_THETA = 10000.0
SGU_WIDTH = 2 * D_MODEL
SGU_GROUPS = 8
SGU_GROUP_DIM = SGU_WIDTH // SGU_GROUPS
CHUNK = 128
N_GROUPS = 4
EXPERTS_PER_GROUP = 8
N_EXPERTS = N_GROUPS * EXPERTS_PER_GROUP
EXPERT_FF = D_MODEL // 4
EPS = 1e-6

LANES = 128
SUBLANES = 8
N_PAIRS = EXPERTS_PER_GROUP * (EXPERTS_PER_GROUP - 1) // 2
N_CLASSES = N_GROUPS * N_PAIRS
ROW_WIDTH = D_MODEL + LANES
META_WA, META_WB, META_CLS, META_RSTD = 0, 1, 2, 3

VMEM_LIMIT = 56 * 1024 * 1024

TM_QKV = 512
TQ_ATTN = 128
TM_ROUTER = 512
TM_MOE = 256
TM_SGU = 256

_PAIR_A = np.array([a for a in range(EXPERTS_PER_GROUP) for b in range(a + 1, EXPERTS_PER_GROUP)], np.int32)
_PAIR_B = np.array([b for a in range(EXPERTS_PER_GROUP) for b in range(a + 1, EXPERTS_PER_GROUP)], np.int32)
_CLASS_EA = np.concatenate([g * EXPERTS_PER_GROUP + _PAIR_A for g in range(N_GROUPS)]).astype(np.int32)
_CLASS_EB = np.concatenate([g * EXPERTS_PER_GROUP + _PAIR_B for g in range(N_GROUPS)]).astype(np.int32)


def _params(*sem):
    return pltpu.CompilerParams(dimension_semantics=sem, vmem_limit_bytes=VMEM_LIMIT)


def _const_spec(shape):
    nd = len(shape)
    return pl.BlockSpec(shape, lambda *_: (0,) * nd, pipeline_mode=pl.Buffered(1))


def _rms_scale(x):
    return lax.rsqrt(jnp.mean(x * x, axis=-1, keepdims=True) + EPS)


def _qkv_body(x_ref, g_ref, w_ref, qg_ref, kg_ref, cos_ref, sin_ref, q_ref, k_ref, v_ref, *, q_scale):
    x = x_ref[...]
    h = (x * _rms_scale(x) * g_ref[...]).astype(BF16)
    qkv = jnp.dot(h, w_ref[...], preferred_element_type=F32)
    cos = cos_ref[...]
    sin = sin_ref[...]
    lane = lax.broadcasted_iota(jnp.int32, (x.shape[0], HEAD_DIM), 1)
    is_x1 = (lane % (2 * ROPE_PAIRS)) < ROPE_PAIRS

    def norm_rope(xh, gain):
        y = xh * _rms_scale(xh) * gain
        partner = jnp.where(is_x1, pltpu.roll(y, HEAD_DIM - ROPE_PAIRS, 1), pltpu.roll(y, ROPE_PAIRS, 1))
        return y * cos + partner * sin

    for hd in range(N_HEADS):
        sl = slice(hd * HEAD_DIM, (hd + 1) * HEAD_DIM)
        q_ref[:, sl] = (norm_rope(qkv[:, sl], qg_ref[...]) * q_scale).astype(BF16)
    for hd in range(N_KV_HEADS):
        sl = slice(Q_WIDTH + hd * HEAD_DIM, Q_WIDTH + (hd + 1) * HEAD_DIM)
        k_ref[:, hd * HEAD_DIM:(hd + 1) * HEAD_DIM] = norm_rope(qkv[:, sl], kg_ref[...]).astype(BF16)
    ones = jnp.ones((x.shape[0], HEAD_DIM), BF16)
    for hd in range(N_KV_HEADS):
        sl = slice(Q_WIDTH + KV_WIDTH + hd * HEAD_DIM, Q_WIDTH + KV_WIDTH + (hd + 1) * HEAD_DIM)
        v_ref[:, 2 * hd * HEAD_DIM:(2 * hd + 1) * HEAD_DIM] = qkv[:, sl].astype(BF16)
        v_ref[:, (2 * hd + 1) * HEAD_DIM:(2 * hd + 2) * HEAD_DIM] = ones


def _rope_tables(S):
    t = jnp.arange(S, dtype=jnp.int32)
    row = (t // GRID_W).astype(F32)
    col = (t % GRID_W).astype(F32)
    inv = ROPE_THETA ** (-jnp.arange(ROPE_PAIRS, dtype=F32) / ROPE_PAIRS)
    ar = row[:, None] * inv
    ac = col[:, None] * inv
    cos = jnp.concatenate([jnp.cos(ar), jnp.cos(ar), jnp.cos(ac), jnp.cos(ac)], axis=1)
    sin = jnp.concatenate([-jnp.sin(ar), jnp.sin(ar), -jnp.sin(ac), jnp.sin(ac)], axis=1)
    return cos, sin


def _qkv_call(x, norm_g, w_in, q_gain, k_gain, S):
    T = x.shape[0]
    tm = TM_QKV
    nseq = S // tm
    cos, sin = _rope_tables(S)
    q_scale = (HEAD_DIM ** -0.5) * math.log2(math.e)
    row = lambda i: (i, 0)
    pos = lambda i: (i % nseq, 0)
    return pl.pallas_call(
        functools.partial(_qkv_body, q_scale=q_scale),
        grid=(T // tm,),
        in_specs=[
            pl.BlockSpec((tm, D_MODEL), row),
            _const_spec((1, D_MODEL)),
            _const_spec((D_MODEL, QKV_WIDTH)),
            _const_spec((1, HEAD_DIM)),
            _const_spec((1, HEAD_DIM)),
            pl.BlockSpec((tm, HEAD_DIM), pos),
            pl.BlockSpec((tm, HEAD_DIM), pos),
        ],
        out_specs=[
            pl.BlockSpec((tm, Q_WIDTH), row),
            pl.BlockSpec((tm, KV_WIDTH), row),
            pl.BlockSpec((tm, 2 * KV_WIDTH), row),
        ],
        out_shape=[
            jax.ShapeDtypeStruct((T, Q_WIDTH), BF16),
            jax.ShapeDtypeStruct((T, KV_WIDTH), BF16),
            jax.ShapeDtypeStruct((T, 2 * KV_WIDTH), BF16),
        ],
        compiler_params=_params("parallel"),
        name="qkv_rope",
    )(x, norm_g, w_in, q_gain, k_gain, cos, sin)


def _attn_body(q_ref, k_ref, v_ref, x_ref, wo_ref, o_ref, heads_ref):
    tq = q_ref.shape[0]
    for h in range(N_KV_HEADS):
        kh = k_ref[:, h * HEAD_DIM:(h + 1) * HEAD_DIM]
        vh = v_ref[:, 2 * h * HEAD_DIM:(2 * h + 2) * HEAD_DIM]
        q4 = jnp.concatenate(
            [q_ref[:, (Q_PER_KV * h + g) * HEAD_DIM:(Q_PER_KV * h + g + 1) * HEAD_DIM] for g in range(Q_PER_KV)],
            axis=0)
        s = lax.dot_general(q4, kh, (((1,), (1,)), ((), ())), preferred_element_type=F32)
        m = jnp.max(s, axis=-1, keepdims=True)
        p = jnp.exp2(s - m).astype(BF16)
        pv = jnp.dot(p, vh, preferred_element_type=F32)
        o = pv[:, :HEAD_DIM] / pv[:, HEAD_DIM:]
        for g in range(Q_PER_KV):
            hd = Q_PER_KV * h + g
            heads_ref[:, hd * HEAD_DIM:(hd + 1) * HEAD_DIM] = o[g * tq:(g + 1) * tq].astype(BF16)
    o_ref[...] = x_ref[...] + jnp.dot(heads_ref[...], wo_ref[...], preferred_element_type=F32)


def _attn_call(q, k, v, x, w_out, B, S):
    T = x.shape[0]
    tq = TQ_ATTN
    nq = S // tq
    row = lambda b, j: (b * nq + j, 0)
    seq = lambda b, j: (b, 0)
    return pl.pallas_call(
        _attn_body,
        grid=(B, nq),
        in_specs=[
            pl.BlockSpec((tq, Q_WIDTH), row),
            pl.BlockSpec((S, KV_WIDTH), seq),
            pl.BlockSpec((S, 2 * KV_WIDTH), seq),
            pl.BlockSpec((tq, D_MODEL), row),
            _const_spec((Q_WIDTH, D_MODEL)),
        ],
        out_specs=pl.BlockSpec((tq, D_MODEL), row),
        out_shape=jax.ShapeDtypeStruct((T, D_MODEL), F32),
        scratch_shapes=[pltpu.VMEM((tq, Q_WIDTH), BF16)],
        compiler_params=_params("parallel", "parallel"),
        name="attention",
    )(q, k, v, x, w_out)


def _router_body(x_ref, g_ref, whi_ref, wlo_ref, b_ref, xr_ref):
    x = x_ref[...]
    tm = x.shape[0]
    rstd = _rms_scale(x)
    t = x * rstd * g_ref[...]
    thi = t.astype(BF16)
    tlo = (t - thi.astype(F32)).astype(BF16)
    whi = whi_ref[...]
    logits = (jnp.dot(thi, whi, preferred_element_type=F32)
              + jnp.dot(tlo, whi, preferred_element_type=F32)
              + jnp.dot(thi, wlo_ref[...], preferred_element_type=F32)
              + b_ref[...])
    lane = lax.broadcasted_iota(jnp.int32, (tm, LANES), 1)
    neg = jnp.float32(-jnp.inf)

    def first_argmax(vals):
        top = jnp.max(vals, axis=-1, keepdims=True)
        idx = jnp.min(jnp.where(vals == top, lane, LANES), axis=-1, keepdims=True)
        return top, idx

    is_group = lane < N_GROUPS
    gl = jnp.where(is_group, logits, neg)
    g_top, g_idx = first_argmax(gl)
    g_val = 1.0 / jnp.sum(jnp.where(is_group, jnp.exp(gl - g_top), 0.0), axis=-1, keepdims=True)
    e_base = N_GROUPS + g_idx * EXPERTS_PER_GROUP
    el = jnp.where((lane >= e_base) & (lane < e_base + EXPERTS_PER_GROUP), logits, neg)
    l1, i1 = first_argmax(el)
    l2, i2 = first_argmax(jnp.where(lane == i1, neg, el))
    r = jnp.exp(l2 - l1)
    w1 = g_val / (1.0 + r)
    w2 = g_val * r / (1.0 + r)
    first_is_low = i1 < i2
    a = jnp.minimum(i1, i2) - e_base
    b = jnp.maximum(i1, i2) - e_base
    wa = jnp.where(first_is_low, w1, w2)
    wb = jnp.where(first_is_low, w2, w1)
    pair = (a * (2 * EXPERTS_PER_GROUP - 1 - a)) // 2 + (b - a - 1)
    cls = (g_idx * N_PAIRS + pair).astype(F32)
    meta = jnp.where(lane == META_WA, wa,
                     jnp.where(lane == META_WB, wb,
                               jnp.where(lane == META_CLS, cls,
                                         jnp.where(lane == META_RSTD, rstd, 0.0))))
    xr_ref[:, :D_MODEL] = x
    xr_ref[:, D_MODEL:] = meta


def _router_call(x, norm_g, w_group, b_group, w_router, b_router):
    T = x.shape[0]
    tm = TM_ROUTER
    w = jnp.concatenate([w_group, w_router.reshape(D_MODEL, N_EXPERTS)], axis=1)
    w = jnp.pad(w, ((0, 0), (0, LANES - w.shape[1])))
    whi = w.astype(BF16)
    wlo = (w - whi.astype(F32)).astype(BF16)
    b = jnp.concatenate([b_group, b_router.reshape(N_EXPERTS)])
    b = jnp.pad(b, (0, LANES - b.shape[0])).reshape(1, LANES)
    row = lambda i: (i, 0)
    return pl.pallas_call(
        _router_body,
        grid=(T // tm,),
        in_specs=[
            pl.BlockSpec((tm, D_MODEL), row),
            _const_spec((1, D_MODEL)),
            _const_spec((D_MODEL, LANES)),
            _const_spec((D_MODEL, LANES)),
            _const_spec((1, LANES)),
        ],
        out_specs=pl.BlockSpec((tm, ROW_WIDTH), row),
        out_shape=jax.ShapeDtypeStruct((T, ROW_WIDTH), F32),
        compiler_params=_params("parallel"),
        name="router",
    )(x, norm_g, whi, wlo, b)


def _route_tiles(cls, T, tm):
    nt = T // tm + N_CLASSES
    key = cls * T + jnp.arange(T, dtype=jnp.int32)
    skey = jnp.sort(key)
    order = skey % T
    bounds = jnp.searchsorted(skey, jnp.arange(N_CLASSES + 1, dtype=jnp.int32) * T).astype(jnp.int32)
    starts = bounds[:-1]
    counts = bounds[1:] - bounds[:-1]
    ntiles = (counts + tm - 1) // tm
    tile_end = jnp.cumsum(ntiles)
    tile_beg = tile_end - ntiles
    total = tile_end[-1]
    ti = jnp.arange(nt, dtype=jnp.int32)
    c = jnp.searchsorted(tile_end, jnp.minimum(ti, total - 1), side="right").astype(jnp.int32)
    j = ti - tile_beg[c]
    src = starts[c] + j * tm
    n = jnp.where(ti < total, jnp.clip(counts[c] - j * tm, 0, tm), 0)
    ea = jnp.asarray(_CLASS_EA)[c]
    eb = jnp.asarray(_CLASS_EB)[c]
    return order, ea, eb, src.astype(jnp.int32), n.astype(jnp.int32)


def _moe_body(ea_ref, eb_ref, src_ref, n_ref, order_ref,
              xr_hbm, g_ref, wga_ref, wgb_ref, wua_ref, wub_ref, wda_ref, wdb_ref,
              out_hbm, xbuf, obuf, gsem, ssem, *, n_tokens):
    del ea_ref, eb_ref
    i = pl.program_id(0)
    tm = xbuf.shape[0]
    n = n_ref[i]

    @pl.when(n > 0)
    def _():
        src = src_ref[i]

        def gather_row(r, carry):
            tok = order_ref[jnp.minimum(src + r, n_tokens - 1)]
            pltpu.make_async_copy(xr_hbm.at[pl.ds(tok, 1)], xbuf.at[pl.ds(r, 1)], gsem).start()
            return carry

        lax.fori_loop(0, tm, gather_row, 0)
        pltpu.make_async_copy(xr_hbm.at[pl.ds(0, tm)], xbuf, gsem).wait()

        x = xbuf[:, :D_MODEL]
        meta = xbuf[:, D_MODEL:]
        wa = meta[:, META_WA:META_WA + 1]
        wb = meta[:, META_WB:META_WB + 1]
        rstd = meta[:, META_RSTD:META_RSTD + 1]
        t = (x * rstd * g_ref[...]).astype(BF16)

        def expert_hidden(wg_ref, wu_ref, w):
            gate = jnp.dot(t, wg_ref[0], preferred_element_type=F32)
            up = jnp.dot(t, wu_ref[0], preferred_element_type=F32)
            return (jax.nn.silu(gate) * up * w).astype(BF16)

        y = (jnp.dot(expert_hidden(wga_ref, wua_ref, wa), wda_ref[0], preferred_element_type=F32)
             + jnp.dot(expert_hidden(wgb_ref, wub_ref, wb), wdb_ref[0], preferred_element_type=F32))
        obuf[...] = x + y

        def scatter_row(r, carry):
            tok = order_ref[src + r]
            pltpu.make_async_copy(obuf.at[pl.ds(r, 1)], out_hbm.at[pl.ds(tok, 1)], ssem).start()
            return carry

        lax.fori_loop(0, n, scatter_row, 0)
        n_aligned = pl.multiple_of((n // SUBLANES) * SUBLANES, SUBLANES)

        @pl.when(n_aligned > 0)
        def _():
            pltpu.make_async_copy(obuf.at[pl.ds(0, n_aligned)], out_hbm.at[pl.ds(0, n_aligned)], ssem).wait()

        def wait_row(r, carry):
            pltpu.make_async_copy(obuf.at[pl.ds(0, 1)], out_hbm.at[pl.ds(0, 1)], ssem).wait()
            return carry

        lax.fori_loop(n_aligned, n, wait_row, 0)


def _moe_call(xr, norm_g, w_gate, w_up, w_down):
    T = xr.shape[0]
    tm = TM_MOE
    cls = xr[:, D_MODEL + META_CLS].astype(jnp.int32)
    order, ea, eb, src, n = _route_tiles(cls, T, tm)
    nt = ea.shape[0]
    w_in_a = lambda i, ea, eb, src, n, order: (ea[i], 0, 0)
    w_in_b = lambda i, ea, eb, src, n, order: (eb[i], 0, 0)
    const2 = lambda i, *_: (0, 0)
    any_spec = pl.BlockSpec(memory_space=pl.ANY)
    grid_spec = pltpu.PrefetchScalarGridSpec(
        num_scalar_prefetch=5,
        grid=(nt,),
        in_specs=[
            any_spec,
            pl.BlockSpec((1, D_MODEL), const2),
            pl.BlockSpec((1, D_MODEL, EXPERT_FF), w_in_a),
            pl.BlockSpec((1, D_MODEL, EXPERT_FF), w_in_b),
            pl.BlockSpec((1, D_MODEL, EXPERT_FF), w_in_a),
            pl.BlockSpec((1, D_MODEL, EXPERT_FF), w_in_b),
            pl.BlockSpec((1, EXPERT_FF, D_MODEL), w_in_a),
            pl.BlockSpec((1, EXPERT_FF, D_MODEL), w_in_b),
        ],
        out_specs=any_spec,
        scratch_shapes=[
            pltpu.VMEM((tm, ROW_WIDTH), F32),
            pltpu.VMEM((tm, D_MODEL), F32),
            pltpu.SemaphoreType.DMA(()),
            pltpu.SemaphoreType.DMA(()),
        ],
    )
    return pl.pallas_call(
        functools.partial(_moe_body, n_tokens=T),
        grid_spec=grid_spec,
        out_shape=jax.ShapeDtypeStruct((T, D_MODEL), F32),
        compiler_params=_params("arbitrary"),
        name="routed_experts",
    )(ea, eb, src, n, order, xr, norm_g, w_gate, w_gate, w_up, w_up, w_down, w_down)


def _sgu_body(x_ref, g_ref, win_ref, vg_ref, ws_ref, bs_ref, wout_ref, o_ref, us_ref):
    x = x_ref[...]
    tm = x.shape[0]
    h = (x * _rms_scale(x) * g_ref[...]).astype(BF16)
    z = jnp.dot(h, win_ref[...], preferred_element_type=F32)
    z = 0.5 * z * (1.0 + lax.erf(z * math.sqrt(0.5)))
    u = z[:, :SGU_WIDTH]
    v = z[:, SGU_WIDTH:]
    v = (v * _rms_scale(v) * vg_ref[...]).astype(BF16)
    for c in range(tm // CHUNK):
        rows = slice(c * CHUNK, (c + 1) * CHUNK)
        for g in range(SGU_GROUPS):
            cols = slice(g * SGU_GROUP_DIM, (g + 1) * SGU_GROUP_DIM)
            s = jnp.dot(ws_ref[g], v[rows, cols], preferred_element_type=F32) + bs_ref[:, g:g + 1]
            us_ref[rows, cols] = (u[rows, cols] * s).astype(BF16)
    o_ref[...] = x + jnp.dot(us_ref[...], wout_ref[...], preferred_element_type=F32)


def _sgu_call(x, norm_g, w_in, v_gain, w_s, b_s, w_out):
    T = x.shape[0]
    tm = TM_SGU
    row = lambda i: (i, 0)
    return pl.pallas_call(
        _sgu_body,
        grid=(T // tm,),
        in_specs=[
            pl.BlockSpec((tm, D_MODEL), row),
            _const_spec((1, D_MODEL)),
            _const_spec((D_MODEL, 2 * SGU_WIDTH)),
            _const_spec((1, SGU_WIDTH)),
            _const_spec((SGU_GROUPS, CHUNK, CHUNK)),
            _const_spec((CHUNK, SGU_GROUPS)),
            _const_spec((SGU_WIDTH, D_MODEL)),
        ],
        out_specs=pl.BlockSpec((tm, D_MODEL), row),
        out_shape=jax.ShapeDtypeStruct((T, D_MODEL), F32),
        scratch_shapes=[pltpu.VMEM((tm, SGU_WIDTH), BF16)],
        compiler_params=_params("parallel"),
        name="sgu",
    )(x, norm_g, w_in, v_gain, w_s, b_s, w_out)


def _encoder(x, p):
    B, S, _ = x.shape
    x = x.reshape(B * S, D_MODEL)
    q, k, v = _qkv_call(x, p["attn_norm"], p["attn_w_in"], p["attn_q_gain"], p["attn_k_gain"], S)
    x = _attn_call(q, k, v, x, p["attn_w_out"], B, S)
    for layer in range(2):
        if layer == 1:
            x = _sgu_call(x, p["sgu_norm"], p["sgu_w_in"], p["sgu_v_gain"], p["sgu_w_s"], p["sgu_b_s"],
                          p["sgu_w_out"])
        xr = _router_call(x, p["moe_norm"][layer], p["moe_w_group"][layer], p["moe_b_group"][layer],
                          p["moe_w_router"][layer], p["moe_b_router"][layer])
        x = _moe_call(xr, p["moe_norm"][layer], p["moe_w_gate"][layer], p["moe_w_up"][layer],
                      p["moe_w_down"][layer])
    return x.reshape(B, S, D_MODEL)


def kernel(x_prompt, x_sample, attn_norm, attn_w_in, attn_q_gain, attn_k_gain, attn_w_out, sgu_norm, sgu_w_in, sgu_v_gain, sgu_w_s, sgu_b_s, sgu_w_out, moe_norm, moe_w_group, moe_b_group, moe_w_router, moe_b_router, moe_w_gate, moe_w_up, moe_w_down):
    p = {
        "attn_norm": attn_norm[0].reshape(1, D_MODEL),
        "attn_w_in": attn_w_in[0].astype(BF16),
        "attn_q_gain": attn_q_gain[0].reshape(1, HEAD_DIM),
        "attn_k_gain": attn_k_gain[0].reshape(1, HEAD_DIM),
        "attn_w_out": attn_w_out[0].astype(BF16),
        "sgu_norm": sgu_norm[0].reshape(1, D_MODEL),
        "sgu_w_in": sgu_w_in[0].astype(BF16),
        "sgu_v_gain": sgu_v_gain[0].reshape(1, SGU_WIDTH),
        "sgu_w_s": sgu_w_s[0].astype(BF16),
        "sgu_b_s": sgu_b_s[0].T,
        "sgu_w_out": sgu_w_out[0].astype(BF16),
        "moe_norm": moe_norm.reshape(2, 1, D_MODEL),
        "moe_w_group": moe_w_group,
        "moe_b_group": moe_b_group,
        "moe_w_router": moe_w_router,
        "moe_b_router": moe_b_router,
        "moe_w_gate": moe_w_gate.astype(BF16),
        "moe_w_up": moe_w_up.astype(BF16),
        "moe_w_down": moe_w_down.astype(BF16),
    }
    return _encoder(x_prompt, p), _encoder(x_sample, p)
```

```python
import functools
import math

import numpy as np
import jax
import jax.numpy as jnp
from jax import lax
from jax.experimental import pallas as pl
from jax.experimental.pallas import tpu as pltpu

F32 = jnp.float32
BF16 = jnp.bfloat16

D_MODEL = 1024
GRID_W = 64
HEAD_DIM = 128
N_HEADS = 8
N_KV_HEADS = 2
Q_PER_KV = N_HEADS // N_KV_HEADS
Q_WIDTH = N_HEADS * HEAD_DIM
KV_WIDTH = N_KV_HEADS * HEAD_DIM
QKV_WIDTH = Q_WIDTH + 2 * KV_WIDTH
ROPE_PAIRS = HEAD_DIM // 4
ROPE_THETA = 10000.0
SGU_WIDTH = 2 * D_MODEL
SGU_GROUPS = 8
SGU_GROUP_DIM = SGU_WIDTH // SGU_GROUPS
CHUNK = 128
N_GROUPS = 4
EXPERTS_PER_GROUP = 8
N_EXPERTS = N_GROUPS * EXPERTS_PER_GROUP
EXPERT_FF = D_MODEL // 4
EPS = 1e-6

LANES = 128
SUBLANES = 8
N_PAIRS = EXPERTS_PER_GROUP * (EXPERTS_PER_GROUP - 1) // 2
N_CLASSES = N_GROUPS * N_PAIRS
ROW_WIDTH = D_MODEL + LANES
META_WA, META_WB, META_CLS, META_RSTD = 0, 1, 2, 3

VMEM_LIMIT = 56 * 1024 * 1024

TM_QKV = 512
TQ_ATTN = 128
TM_MOE = 256
TM_SGU = 256
MOE_ISSUE_GROUPS = 6

_PAIR_A = np.array([a for a in range(EXPERTS_PER_GROUP) for b in range(a + 1, EXPERTS_PER_GROUP)], np.int32)
_PAIR_B = np.array([b for a in range(EXPERTS_PER_GROUP) for b in range(a + 1, EXPERTS_PER_GROUP)], np.int32)
_CLASS_EA = np.concatenate([g * EXPERTS_PER_GROUP + _PAIR_A for g in range(N_GROUPS)]).astype(np.int32)
_CLASS_EB = np.concatenate([g * EXPERTS_PER_GROUP + _PAIR_B for g in range(N_GROUPS)]).astype(np.int32)


def _params(*sem):
    return pltpu.CompilerParams(dimension_semantics=sem, vmem_limit_bytes=VMEM_LIMIT)


def _const_spec(shape):
    nd = len(shape)
    return pl.BlockSpec(shape, lambda *_: (0,) * nd, pipeline_mode=pl.Buffered(1))


def _rms_scale(x):
    return lax.rsqrt(jnp.mean(x * x, axis=-1, keepdims=True) + EPS)


def _qkv_body(x_ref, g_ref, w_ref, qg_ref, kg_ref, cos_ref, sin_ref, q_ref, k_ref, v_ref, *, q_scale):
    x = x_ref[...]
    h = (x * _rms_scale(x) * g_ref[...]).astype(BF16)
    qkv = jnp.dot(h, w_ref[...], preferred_element_type=F32)
    cos = cos_ref[...]
    sin = sin_ref[...]
    lane = lax.broadcasted_iota(jnp.int32, (x.shape[0], HEAD_DIM), 1)
    is_x1 = (lane % (2 * ROPE_PAIRS)) < ROPE_PAIRS

    def norm_rope(xh, gain):
        y = xh * _rms_scale(xh) * gain
        partner = jnp.where(is_x1, pltpu.roll(y, HEAD_DIM - ROPE_PAIRS, 1), pltpu.roll(y, ROPE_PAIRS, 1))
        return y * cos + partner * sin

    for hd in range(N_HEADS):
        sl = slice(hd * HEAD_DIM, (hd + 1) * HEAD_DIM)
        q_ref[:, sl] = (norm_rope(qkv[:, sl], qg_ref[...]) * q_scale).astype(BF16)
    for hd in range(N_KV_HEADS):
        sl = slice(Q_WIDTH + hd * HEAD_DIM, Q_WIDTH + (hd + 1) * HEAD_DIM)
        k_ref[:, hd * HEAD_DIM:(hd + 1) * HEAD_DIM] = norm_rope(qkv[:, sl], kg_ref[...]).astype(BF16)
    ones = jnp.ones((x.shape[0], HEAD_DIM), BF16)
    for hd in range(N_KV_HEADS):
        sl = slice(Q_WIDTH + KV_WIDTH + hd * HEAD_DIM, Q_WIDTH + KV_WIDTH + (hd + 1) * HEAD_DIM)
        v_ref[:, 2 * hd * HEAD_DIM:(2 * hd + 1) * HEAD_DIM] = qkv[:, sl].astype(BF16)
        v_ref[:, (2 * hd + 1) * HEAD_DIM:(2 * hd + 2) * HEAD_DIM] = ones


def _rope_tables(S):
    t = jnp.arange(S, dtype=jnp.int32)
    row = (t // GRID_W).astype(F32)
    col = (t % GRID_W).astype(F32)
    inv = ROPE_THETA ** (-jnp.arange(ROPE_PAIRS, dtype=F32) / ROPE_PAIRS)
    ar = row[:, None] * inv
    ac = col[:, None] * inv
    cos = jnp.concatenate([jnp.cos(ar), jnp.cos(ar), jnp.cos(ac), jnp.cos(ac)], axis=1)
    sin = jnp.concatenate([-jnp.sin(ar), jnp.sin(ar), -jnp.sin(ac), jnp.sin(ac)], axis=1)
    return cos, sin


def _qkv_call(x, norm_g, w_in, q_gain, k_gain, S):
    T = x.shape[0]
    tm = TM_QKV
    nseq = S // tm
    cos, sin = _rope_tables(S)
    q_scale = (HEAD_DIM ** -0.5) * math.log2(math.e)
    row = lambda i: (i, 0)
    pos = lambda i: (i % nseq, 0)
    return pl.pallas_call(
        functools.partial(_qkv_body, q_scale=q_scale),
        grid=(T // tm,),
        in_specs=[
            pl.BlockSpec((tm, D_MODEL), row),
            _const_spec((1, D_MODEL)),
            _const_spec((D_MODEL, QKV_WIDTH)),
            _const_spec((1, HEAD_DIM)),
            _const_spec((1, HEAD_DIM)),
            pl.BlockSpec((tm, HEAD_DIM), pos),
            pl.BlockSpec((tm, HEAD_DIM), pos),
        ],
        out_specs=[
            pl.BlockSpec((tm, Q_WIDTH), row),
            pl.BlockSpec((tm, KV_WIDTH), row),
            pl.BlockSpec((tm, 2 * KV_WIDTH), row),
        ],
        out_shape=[
            jax.ShapeDtypeStruct((T, Q_WIDTH), BF16),
            jax.ShapeDtypeStruct((T, KV_WIDTH), BF16),
            jax.ShapeDtypeStruct((T, 2 * KV_WIDTH), BF16),
        ],
        compiler_params=_params("parallel"),
        name="qkv_rope",
    )(x, norm_g, w_in, q_gain, k_gain, cos, sin)


def _route_rows(x, g_ref, whi_ref, wlo_ref, b_ref):
    tm = x.shape[0]
    rstd = _rms_scale(x)
    t = x * rstd * g_ref[...]
    thi = t.astype(BF16)
    tlo = (t - thi.astype(F32)).astype(BF16)
    whi = whi_ref[...]
    logits = (jnp.dot(thi, whi, preferred_element_type=F32)
              + jnp.dot(tlo, whi, preferred_element_type=F32)
              + jnp.dot(thi, wlo_ref[...], preferred_element_type=F32)
              + b_ref[...])
    lane = lax.broadcasted_iota(jnp.int32, (tm, LANES), 1)
    neg = jnp.float32(-jnp.inf)

    def first_argmax(vals):
        top = jnp.max(vals, axis=-1, keepdims=True)
        idx = jnp.min(jnp.where(vals == top, lane, LANES), axis=-1, keepdims=True)
        return top, idx

    is_group = lane < N_GROUPS
    gl = jnp.where(is_group, logits, neg)
    g_top, g_idx = first_argmax(gl)
    g_val = 1.0 / jnp.sum(jnp.where(is_group, jnp.exp(gl - g_top), 0.0), axis=-1, keepdims=True)
    e_base = N_GROUPS + g_idx * EXPERTS_PER_GROUP
    el = jnp.where((lane >= e_base) & (lane < e_base + EXPERTS_PER_GROUP), logits, neg)
    l1, i1 = first_argmax(el)
    l2, i2 = first_argmax(jnp.where(lane == i1, neg, el))
    r = jnp.exp(l2 - l1)
    w1 = g_val / (1.0 + r)
    w2 = g_val * r / (1.0 + r)
    first_is_low = i1 < i2
    a = jnp.minimum(i1, i2) - e_base
    b = jnp.maximum(i1, i2) - e_base
    wa = jnp.where(first_is_low, w1, w2)
    wb = jnp.where(first_is_low, w2, w1)
    pair = (a * (2 * EXPERTS_PER_GROUP - 1 - a)) // 2 + (b - a - 1)
    cls = (g_idx * N_PAIRS + pair).astype(F32)
    return jnp.where(lane == META_WA, wa,
                     jnp.where(lane == META_WB, wb,
                               jnp.where(lane == META_CLS, cls,
                                         jnp.where(lane == META_RSTD, rstd, 0.0))))


def _router_operands(norm_g, w_group, b_group, w_router, b_router):
    w = jnp.concatenate([w_group, w_router.reshape(D_MODEL, N_EXPERTS)], axis=1)
    w = jnp.pad(w, ((0, 0), (0, LANES - w.shape[1])))
    whi = w.astype(BF16)
    wlo = (w - whi.astype(F32)).astype(BF16)
    b = jnp.concatenate([b_group, b_router.reshape(N_EXPERTS)])
    b = jnp.pad(b, (0, LANES - b.shape[0])).reshape(1, LANES)
    return norm_g, whi, wlo, b


_ROUTER_SPECS = [(1, D_MODEL), (D_MODEL, LANES), (D_MODEL, LANES), (1, LANES)]


def _route_tiles(cls, T, tm):
    nt = T // tm + N_CLASSES
    key = cls * T + jnp.arange(T, dtype=jnp.int32)
    skey = jnp.sort(key)
    order = skey % T
    class_lo = jnp.arange(N_CLASSES + 1, dtype=jnp.int32) * T
    bounds = jnp.sum((skey[None, :] < class_lo[:, None]).astype(jnp.int32), axis=1)
    starts = bounds[:-1]
    counts = bounds[1:] - bounds[:-1]
    ntiles = (counts + tm - 1) // tm
    tile_end = jnp.cumsum(ntiles)
    tile_beg = tile_end - ntiles
    total = tile_end[-1]
    ti = jnp.arange(nt, dtype=jnp.int32)
    tc = jnp.minimum(ti, total - 1)
    c = jnp.sum((tile_end[None, :] <= tc[:, None]).astype(jnp.int32), axis=1)
    j = ti - tile_beg[c]
    src = starts[c] + j * tm
    n = jnp.where(ti < total, jnp.clip(counts[c] - j * tm, 0, tm), 0)
    ea = jnp.asarray(_CLASS_EA)[c]
    eb = jnp.asarray(_CLASS_EB)[c]
    return order, ea, eb, src.astype(jnp.int32), n.astype(jnp.int32)


def _attn_body(q_ref, k_ref, v_ref, x_ref, wo_ref, rg_ref, rhi_ref, rlo_ref, rb_ref, o_ref, heads_ref):
    tq = q_ref.shape[0]
    for h in range(N_KV_HEADS):
        kh = k_ref[:, h * HEAD_DIM:(h + 1) * HEAD_DIM]
        vh = v_ref[:, 2 * h * HEAD_DIM:(2 * h + 2) * HEAD_DIM]
        q4 = jnp.concatenate(
            [q_ref[:, (Q_PER_KV * h + g) * HEAD_DIM:(Q_PER_KV * h + g + 1) * HEAD_DIM] for g in range(Q_PER_KV)],
            axis=0)
        s = lax.dot_general(q4, kh, (((1,), (1,)), ((), ())), preferred_element_type=F32)
        m = jnp.max(s, axis=-1, keepdims=True)
        p = jnp.exp2(s - m).astype(BF16)
        pv = jnp.dot(p, vh, preferred_element_type=F32)
        o = pv[:, :HEAD_DIM] / pv[:, HEAD_DIM:]
        for g in range(Q_PER_KV):
            hd = Q_PER_KV * h + g
            heads_ref[:, hd * HEAD_DIM:(hd + 1) * HEAD_DIM] = o[g * tq:(g + 1) * tq].astype(BF16)
    x = x_ref[...] + jnp.dot(heads_ref[...], wo_ref[...], preferred_element_type=F32)
    o_ref[:, :D_MODEL] = x
    o_ref[:, D_MODEL:] = _route_rows(x, rg_ref, rhi_ref, rlo_ref, rb_ref)


def _attn_call(q, k, v, x, w_out, router, B, S):
    T = x.shape[0]
    tq = TQ_ATTN
    nq = S // tq
    row = lambda b, j: (b * nq + j, 0)
    seq = lambda b, j: (b, 0)
    return pl.pallas_call(
        _attn_body,
        grid=(B, nq),
        in_specs=[
            pl.BlockSpec((tq, Q_WIDTH), row),
            pl.BlockSpec((S, KV_WIDTH), seq),
            pl.BlockSpec((S, 2 * KV_WIDTH), seq),
            pl.BlockSpec((tq, D_MODEL), row),
            _const_spec((Q_WIDTH, D_MODEL)),
        ] + [_const_spec(s) for s in _ROUTER_SPECS],
        out_specs=pl.BlockSpec((tq, ROW_WIDTH), row),
        out_shape=jax.ShapeDtypeStruct((T, ROW_WIDTH), F32),
        scratch_shapes=[pltpu.VMEM((tq, Q_WIDTH), BF16)],
        compiler_params=_params("parallel", "parallel"),
        name="attention",
    )(q, k, v, x, w_out, *router)


def _moe_body(ea_ref, eb_ref, src_ref, n_ref, order_ref,
              xr_hbm, g_ref, wga_ref, wgb_ref, wua_ref, wub_ref, wda_ref, wdb_ref,
              out_hbm, xbuf, obuf, gsem, ssem, *, n_tokens):
    del ea_ref, eb_ref
    i = pl.program_id(0)
    tm = xbuf.shape[1]
    slot = i % 2
    other = 1 - slot
    n = n_ref[i]
    n_prev = jnp.where(i > 0, n_ref[jnp.maximum(i - 1, 0)], 0)
    n_prev2 = jnp.where(i > 1, n_ref[jnp.maximum(i - 2, 0)], 0)
    src_prev = src_ref[jnp.maximum(i - 1, 0)]
    src_next = src_ref[jnp.minimum(i + 1, pl.num_programs(0) - 1)]

    def token(pos):
        return order_ref[jnp.minimum(pos, n_tokens - 1)]

    def gather_copy(tok, r, to_slot):
        return pltpu.make_async_copy(xr_hbm.at[pl.ds(tok, 1)], xbuf.at[to_slot, pl.ds(r, 1)], gsem.at[to_slot])

    def scatter_copy(tok, r, from_slot):
        return pltpu.make_async_copy(obuf.at[from_slot, pl.ds(r, 1)], out_hbm.at[pl.ds(tok, 1)], ssem.at[from_slot])

    def wait_gather(in_slot):
        pltpu.make_async_copy(xr_hbm.at[pl.ds(0, tm)], xbuf.at[in_slot], gsem.at[in_slot]).wait()

    def wait_scatter(from_slot, rows):
        aligned = pl.multiple_of((rows // SUBLANES) * SUBLANES, SUBLANES)

        @pl.when(aligned > 0)
        def _():
            pltpu.make_async_copy(obuf.at[from_slot, pl.ds(0, aligned)], out_hbm.at[pl.ds(0, aligned)],
                                  ssem.at[from_slot]).wait()

        def wait_row(r, carry):
            scatter_copy(0, 0, from_slot).wait()
            return carry

        lax.fori_loop(aligned, rows, wait_row, 0)

    @pl.when((i == 0) & (n > 0))
    def _():
        src = src_ref[0]

        def first_gather(r, carry):
            gather_copy(token(src + r), r, 0).start()
            return carry

        lax.fori_loop(0, tm, first_gather, 0)

    @pl.when(n > 0)
    def _():
        wait_gather(slot)
        wait_scatter(slot, n_prev2)

        def issue_rows(group):
            for r in range(tm * group // MOE_ISSUE_GROUPS, tm * (group + 1) // MOE_ISSUE_GROUPS):
                gather_copy(token(src_next + r), r, other).start()

                @pl.when(r < n_prev)
                def _():
                    scatter_copy(token(src_prev + r), r, other).start()

        x = xbuf[slot, :, :D_MODEL]
        meta = xbuf[slot, :, D_MODEL:]
        wa = meta[:, META_WA:META_WA + 1]
        wb = meta[:, META_WB:META_WB + 1]
        rstd = meta[:, META_RSTD:META_RSTD + 1]
        t = (x * rstd * g_ref[...]).astype(BF16)
        gate_a = jnp.dot(t, wga_ref[0], preferred_element_type=F32)
        issue_rows(0)
        up_a = jnp.dot(t, wua_ref[0], preferred_element_type=F32)
        issue_rows(1)
        h_a = (jax.nn.silu(gate_a) * up_a * wa).astype(BF16)
        gate_b = jnp.dot(t, wgb_ref[0], preferred_element_type=F32)
        issue_rows(2)
        up_b = jnp.dot(t, wub_ref[0], preferred_element_type=F32)
        issue_rows(3)
        h_b = (jax.nn.silu(gate_b) * up_b * wb).astype(BF16)
        y = jnp.dot(h_a, wda_ref[0], preferred_element_type=F32)
        issue_rows(4)
        y = y + jnp.dot(h_b, wdb_ref[0], preferred_element_type=F32)
        issue_rows(5)
        obuf[slot] = x + y

    @pl.when((n == 0) & (n_prev > 0))
    def _():
        wait_gather(slot)

        def last_scatter(r, carry):
            scatter_copy(token(src_prev + r), r, other).start()
            return carry

        lax.fori_loop(0, n_prev, last_scatter, 0)
        wait_scatter(other, n_prev)
        wait_scatter(slot, n_prev2)


def _moe_call(xr, norm_g, w_gate, w_up, w_down):
    T = xr.shape[0]
    tm = TM_MOE
    cls = xr[:, D_MODEL + META_CLS].astype(jnp.int32)
    order, ea, eb, src, n = _route_tiles(cls, T, tm)
    nt = ea.shape[0]
    w_in_a = lambda i, ea, eb, src, n, order: (ea[i], 0, 0)
    w_in_b = lambda i, ea, eb, src, n, order: (eb[i], 0, 0)
    const2 = lambda i, *_: (0, 0)
    any_spec = pl.BlockSpec(memory_space=pl.ANY)
    grid_spec = pltpu.PrefetchScalarGridSpec(
        num_scalar_prefetch=5,
        grid=(nt,),
        in_specs=[
            any_spec,
            pl.BlockSpec((1, D_MODEL), const2),
            pl.BlockSpec((1, D_MODEL, EXPERT_FF), w_in_a),
            pl.BlockSpec((1, D_MODEL, EXPERT_FF), w_in_b),
            pl.BlockSpec((1, D_MODEL, EXPERT_FF), w_in_a),
            pl.BlockSpec((1, D_MODEL, EXPERT_FF), w_in_b),
            pl.BlockSpec((1, EXPERT_FF, D_MODEL), w_in_a),
            pl.BlockSpec((1, EXPERT_FF, D_MODEL), w_in_b),
        ],
        out_specs=any_spec,
        scratch_shapes=[
            pltpu.VMEM((2, tm, ROW_WIDTH), F32),
            pltpu.VMEM((2, tm, D_MODEL), F32),
            pltpu.SemaphoreType.DMA((2,)),
            pltpu.SemaphoreType.DMA((2,)),
        ],
    )
    return pl.pallas_call(
        functools.partial(_moe_body, n_tokens=T),
        grid_spec=grid_spec,
        out_shape=jax.ShapeDtypeStruct((T, D_MODEL), F32),
        compiler_params=_params("arbitrary"),
        name="routed_experts",
    )(ea, eb, src, n, order, xr, norm_g, w_gate, w_gate, w_up, w_up, w_down, w_down)


def _sgu_body(x_ref, g_ref, win_ref, vg_ref, ws_ref, bs_ref, wout_ref, rg_ref, rhi_ref, rlo_ref, rb_ref,
              o_ref, us_ref):
    x = x_ref[...]
    tm = x.shape[0]
    h = (x * _rms_scale(x) * g_ref[...]).astype(BF16)
    z = jnp.dot(h, win_ref[...], preferred_element_type=F32)
    z = 0.5 * z * (1.0 + lax.erf(z * math.sqrt(0.5)))
    u = z[:, :SGU_WIDTH]
    v = z[:, SGU_WIDTH:]
    v = (v * _rms_scale(v) * vg_ref[...]).astype(BF16)
    for c in range(tm // CHUNK):
        rows = slice(c * CHUNK, (c + 1) * CHUNK)
        for g in range(SGU_GROUPS):
            cols = slice(g * SGU_GROUP_DIM, (g + 1) * SGU_GROUP_DIM)
            s = jnp.dot(ws_ref[g], v[rows, cols], preferred_element_type=F32) + bs_ref[:, g:g + 1]
            us_ref[rows, cols] = (u[rows, cols] * s).astype(BF16)
    x = x + jnp.dot(us_ref[...], wout_ref[...], preferred_element_type=F32)
    o_ref[:, :D_MODEL] = x
    o_ref[:, D_MODEL:] = _route_rows(x, rg_ref, rhi_ref, rlo_ref, rb_ref)


def _sgu_call(x, norm_g, w_in, v_gain, w_s, b_s, w_out, router):
    T = x.shape[0]
    tm = TM_SGU
    row = lambda i: (i, 0)
    return pl.pallas_call(
        _sgu_body,
        grid=(T // tm,),
        in_specs=[
            pl.BlockSpec((tm, D_MODEL), row),
            _const_spec((1, D_MODEL)),
            _const_spec((D_MODEL, 2 * SGU_WIDTH)),
            _const_spec((1, SGU_WIDTH)),
            _const_spec((SGU_GROUPS, CHUNK, CHUNK)),
            _const_spec((CHUNK, SGU_GROUPS)),
            _const_spec((SGU_WIDTH, D_MODEL)),
        ] + [_const_spec(s) for s in _ROUTER_SPECS],
        out_specs=pl.BlockSpec((tm, ROW_WIDTH), row),
        out_shape=jax.ShapeDtypeStruct((T, ROW_WIDTH), F32),
        scratch_shapes=[pltpu.VMEM((tm, SGU_WIDTH), BF16)],
        compiler_params=_params("parallel"),
        name="sgu",
    )(x, norm_g, w_in, v_gain, w_s, b_s, w_out, *router)


def _encoder(x, p):
    B, S, _ = x.shape
    x = x.reshape(B * S, D_MODEL)
    q, k, v = _qkv_call(x, p["attn_norm"], p["attn_w_in"], p["attn_q_gain"], p["attn_k_gain"], S)
    xr = _attn_call(q, k, v, x, p["attn_w_out"], p["router"][0], B, S)
    x = _moe_call(xr, p["moe_norm"][0], p["moe_w_gate"][0], p["moe_w_up"][0], p["moe_w_down"][0])
    xr = _sgu_call(x, p["sgu_norm"], p["sgu_w_in"], p["sgu_v_gain"], p["sgu_w_s"], p["sgu_b_s"], p["sgu_w_out"],
                   p["router"][1])
    x = _moe_call(xr, p["moe_norm"][1], p["moe_w_gate"][1], p["moe_w_up"][1], p["moe_w_down"][1])
    return x.reshape(B, S, D_MODEL)


def kernel(x_prompt, x_sample, attn_norm, attn_w_in, attn_q_gain, attn_k_gain, attn_w_out, sgu_norm, sgu_w_in, sgu_v_gain, sgu_w_s, sgu_b_s, sgu_w_out, moe_norm, moe_w_group, moe_b_group, moe_w_router, moe_b_router, moe_w_gate, moe_w_up, moe_w_down):
    moe_norm = moe_norm.reshape(2, 1, D_MODEL)
    p = {
        "attn_norm": attn_norm[0].reshape(1, D_MODEL),
        "attn_w_in": attn_w_in[0].astype(BF16),
        "attn_q_gain": attn_q_gain[0].reshape(1, HEAD_DIM),
        "attn_k_gain": attn_k_gain[0].reshape(1, HEAD_DIM),
        "attn_w_out": attn_w_out[0].astype(BF16),
        "sgu_norm": sgu_norm[0].reshape(1, D_MODEL),
        "sgu_w_in": sgu_w_in[0].astype(BF16),
        "sgu_v_gain": sgu_v_gain[0].reshape(1, SGU_WIDTH),
        "sgu_w_s": sgu_w_s[0].astype(BF16),
        "sgu_b_s": sgu_b_s[0].T,
        "sgu_w_out": sgu_w_out[0].astype(BF16),
        "moe_norm": moe_norm,
        "router": [_router_operands(moe_norm[l], moe_w_group[l], moe_b_group[l], moe_w_router[l], moe_b_router[l])
                   for l in range(2)],
        "moe_w_gate": moe_w_gate.astype(BF16),
        "moe_w_up": moe_w_up.astype(BF16),
        "moe_w_down": moe_w_down.astype(BF16),
    }
    return _encoder(x_prompt, p), _encoder(x_sample, p)
```

```python
import functools
import math

import numpy as np
import jax
import jax.numpy as jnp
from jax import lax
from jax.experimental import pallas as pl
from jax.experimental.pallas import tpu as pltpu

F32 = jnp.float32
BF16 = jnp.bfloat16

D_MODEL = 1024
GRID_W = 64
HEAD_DIM = 128
N_HEADS = 8
N_KV_HEADS = 2
Q_PER_KV = N_HEADS // N_KV_HEADS
Q_WIDTH = N_HEADS * HEAD_DIM
KV_WIDTH = N_KV_HEADS * HEAD_DIM
QKV_WIDTH = Q_WIDTH + 2 * KV_WIDTH
ROPE_PAIRS = HEAD_DIM // 4
ROPE_THETA = 10000.0
SGU_WIDTH = 2 * D_MODEL
SGU_GROUPS = 8
SGU_GROUP_DIM = SGU_WIDTH // SGU_GROUPS
CHUNK = 128
N_GROUPS = 4
EXPERTS_PER_GROUP = 8
N_EXPERTS = N_GROUPS * EXPERTS_PER_GROUP
EXPERT_FF = D_MODEL // 4
EPS = 1e-6

LANES = 128
SUBLANES = 8
N_PAIRS = EXPERTS_PER_GROUP * (EXPERTS_PER_GROUP - 1) // 2
N_CLASSES = N_GROUPS * N_PAIRS
ROW_WIDTH = D_MODEL + LANES
META_WA, META_WB, META_CLS, META_RSTD = 0, 1, 2, 3

VMEM_LIMIT = 56 * 1024 * 1024

TM_QKV = 512
TQ_ATTN = 128
TM_MOE = 256
TM_SGU = 256
MOE_ISSUE_GROUPS = 6

_PAIR_A = np.array([a for a in range(EXPERTS_PER_GROUP) for b in range(a + 1, EXPERTS_PER_GROUP)], np.int32)
_PAIR_B = np.array([b for a in range(EXPERTS_PER_GROUP) for b in range(a + 1, EXPERTS_PER_GROUP)], np.int32)
_CLASS_EA = np.concatenate([g * EXPERTS_PER_GROUP + _PAIR_A for g in range(N_GROUPS)]).astype(np.int32)
_CLASS_EB = np.concatenate([g * EXPERTS_PER_GROUP + _PAIR_B for g in range(N_GROUPS)]).astype(np.int32)


def _params(*sem):
    return pltpu.CompilerParams(dimension_semantics=sem, vmem_limit_bytes=VMEM_LIMIT)


def _const_spec(shape):
    nd = len(shape)
    return pl.BlockSpec(shape, lambda *_: (0,) * nd, pipeline_mode=pl.Buffered(1))


def _rms_scale(x):
    return lax.rsqrt(jnp.mean(x * x, axis=-1, keepdims=True) + EPS)


def _qkv_body(x_ref, g_ref, w_ref, qg_ref, kg_ref, cos_ref, sin_ref, q_ref, k_ref, v_ref, *, q_scale):
    x = x_ref[...]
    h = (x * _rms_scale(x) * g_ref[...]).astype(BF16)
    qkv = jnp.dot(h, w_ref[...], preferred_element_type=F32)
    cos = cos_ref[...]
    sin = sin_ref[...]
    lane = lax.broadcasted_iota(jnp.int32, (x.shape[0], HEAD_DIM), 1)
    is_x1 = (lane % (2 * ROPE_PAIRS)) < ROPE_PAIRS

    def norm_rope(xh, gain):
        y = xh * _rms_scale(xh) * gain
        partner = jnp.where(is_x1, pltpu.roll(y, HEAD_DIM - ROPE_PAIRS, 1), pltpu.roll(y, ROPE_PAIRS, 1))
        return y * cos + partner * sin

    for hd in range(N_HEADS):
        sl = slice(hd * HEAD_DIM, (hd + 1) * HEAD_DIM)
        q_ref[:, sl] = (norm_rope(qkv[:, sl], qg_ref[...]) * q_scale).astype(BF16)
    for hd in range(N_KV_HEADS):
        sl = slice(Q_WIDTH + hd * HEAD_DIM, Q_WIDTH + (hd + 1) * HEAD_DIM)
        k_ref[:, hd * HEAD_DIM:(hd + 1) * HEAD_DIM] = norm_rope(qkv[:, sl], kg_ref[...]).astype(BF16)
    ones = jnp.ones((x.shape[0], HEAD_DIM), BF16)
    for hd in range(N_KV_HEADS):
        sl = slice(Q_WIDTH + KV_WIDTH + hd * HEAD_DIM, Q_WIDTH + KV_WIDTH + (hd + 1) * HEAD_DIM)
        v_ref[:, 2 * hd * HEAD_DIM:(2 * hd + 1) * HEAD_DIM] = qkv[:, sl].astype(BF16)
        v_ref[:, (2 * hd + 1) * HEAD_DIM:(2 * hd + 2) * HEAD_DIM] = ones


def _rope_tables(S):
    t = jnp.arange(S, dtype=jnp.int32)
    row = (t // GRID_W).astype(F32)
    col = (t % GRID_W).astype(F32)
    inv = ROPE_THETA ** (-jnp.arange(ROPE_PAIRS, dtype=F32) / ROPE_PAIRS)
    ar = row[:, None] * inv
    ac = col[:, None] * inv
    cos = jnp.concatenate([jnp.cos(ar), jnp.cos(ar), jnp.cos(ac), jnp.cos(ac)], axis=1)
    sin = jnp.concatenate([-jnp.sin(ar), jnp.sin(ar), -jnp.sin(ac), jnp.sin(ac)], axis=1)
    return cos, sin


def _qkv_call(x, norm_g, w_in, q_gain, k_gain, S):
    T = x.shape[0]
    tm = TM_QKV
    nseq = S // tm
    cos, sin = _rope_tables(S)
    q_scale = (HEAD_DIM ** -0.5) * math.log2(math.e)
    row = lambda i: (i, 0)
    pos = lambda i: (i % nseq, 0)
    return pl.pallas_call(
        functools.partial(_qkv_body, q_scale=q_scale),
        grid=(T // tm,),
        in_specs=[
            pl.BlockSpec((tm, D_MODEL), row),
            _const_spec((1, D_MODEL)),
            _const_spec((D_MODEL, QKV_WIDTH)),
            _const_spec((1, HEAD_DIM)),
            _const_spec((1, HEAD_DIM)),
            pl.BlockSpec((tm, HEAD_DIM), pos),
            pl.BlockSpec((tm, HEAD_DIM), pos),
        ],
        out_specs=[
            pl.BlockSpec((tm, Q_WIDTH), row),
            pl.BlockSpec((tm, KV_WIDTH), row),
            pl.BlockSpec((tm, 2 * KV_WIDTH), row),
        ],
        out_shape=[
            jax.ShapeDtypeStruct((T, Q_WIDTH), BF16),
            jax.ShapeDtypeStruct((T, KV_WIDTH), BF16),
            jax.ShapeDtypeStruct((T, 2 * KV_WIDTH), BF16),
        ],
        compiler_params=_params("parallel"),
        name="qkv_rope",
    )(x, norm_g, w_in, q_gain, k_gain, cos, sin)


def _route_rows(x, g_ref, w_ref, b_ref):
    tm = x.shape[0]
    rstd = _rms_scale(x)
    t = x * rstd * g_ref[...]
    thi = t.astype(BF16)
    tlo = (t - thi.astype(F32)).astype(BF16)
    prod = jnp.dot(jnp.concatenate([thi, tlo], axis=0), w_ref[...], preferred_element_type=F32)
    logits = (prod[:tm, :LANES] + prod[:tm, LANES:] + prod[tm:, :LANES] + prod[tm:, LANES:]
              + b_ref[...])
    lane = lax.broadcasted_iota(jnp.int32, (tm, LANES), 1)
    neg = jnp.float32(-jnp.inf)

    def first_argmax(vals):
        top = jnp.max(vals, axis=-1, keepdims=True)
        idx = jnp.min(jnp.where(vals == top, lane, LANES), axis=-1, keepdims=True)
        return top, idx

    is_group = lane < N_GROUPS
    gl = jnp.where(is_group, logits, neg)
    g_top, g_idx = first_argmax(gl)
    g_val = 1.0 / jnp.sum(jnp.where(is_group, jnp.exp(gl - g_top), 0.0), axis=-1, keepdims=True)
    e_base = N_GROUPS + g_idx * EXPERTS_PER_GROUP
    el = jnp.where((lane >= e_base) & (lane < e_base + EXPERTS_PER_GROUP), logits, neg)
    l1, i1 = first_argmax(el)
    l2, i2 = first_argmax(jnp.where(lane == i1, neg, el))
    r = jnp.exp(l2 - l1)
    w1 = g_val / (1.0 + r)
    w2 = g_val * r / (1.0 + r)
    first_is_low = i1 < i2
    a = jnp.minimum(i1, i2) - e_base
    b = jnp.maximum(i1, i2) - e_base
    wa = jnp.where(first_is_low, w1, w2)
    wb = jnp.where(first_is_low, w2, w1)
    pair = (a * (2 * EXPERTS_PER_GROUP - 1 - a)) // 2 + (b - a - 1)
    cls = (g_idx * N_PAIRS + pair).astype(F32)
    return jnp.where(lane == META_WA, wa,
                     jnp.where(lane == META_WB, wb,
                               jnp.where(lane == META_CLS, cls,
                                         jnp.where(lane == META_RSTD, rstd, 0.0))))


def _router_operands(norm_g, w_group, b_group, w_router, b_router):
    w = jnp.concatenate([w_group, w_router.reshape(D_MODEL, N_EXPERTS)], axis=1)
    w = jnp.pad(w, ((0, 0), (0, LANES - w.shape[1])))
    whi = w.astype(BF16)
    wlo = (w - whi.astype(F32)).astype(BF16)
    b = jnp.concatenate([b_group, b_router.reshape(N_EXPERTS)])
    b = jnp.pad(b, (0, LANES - b.shape[0])).reshape(1, LANES)
    return norm_g, jnp.concatenate([whi, wlo], axis=1), b


_ROUTER_SPECS = [(1, D_MODEL), (D_MODEL, 2 * LANES), (1, LANES)]


def _route_tiles(cls, T, tm):
    nt = T // tm + N_CLASSES
    key = cls * T + jnp.arange(T, dtype=jnp.int32)
    skey = jnp.sort(key)
    order = skey % T
    class_lo = jnp.arange(N_CLASSES + 1, dtype=jnp.int32) * T
    bounds = jnp.sum((skey[None, :] < class_lo[:, None]).astype(jnp.int32), axis=1)
    starts = bounds[:-1]
    counts = bounds[1:] - bounds[:-1]
    ntiles = (counts + tm - 1) // tm
    tile_end = jnp.cumsum(ntiles)
    tile_beg = tile_end - ntiles
    total = tile_end[-1]
    ti = jnp.arange(nt, dtype=jnp.int32)
    tc = jnp.minimum(ti, total - 1)
    c = jnp.sum((tile_end[None, :] <= tc[:, None]).astype(jnp.int32), axis=1)
    j = ti - tile_beg[c]
    src = starts[c] + j * tm
    n = jnp.where(ti < total, jnp.clip(counts[c] - j * tm, 0, tm), 0)
    ea = jnp.asarray(_CLASS_EA)[c]
    eb = jnp.asarray(_CLASS_EB)[c]
    return order, ea, eb, src.astype(jnp.int32), n.astype(jnp.int32)


def _attn_body(q_ref, k_ref, v_ref, x_ref, wo_ref, rg_ref, rw_ref, rb_ref, o_ref, heads_ref):
    tq = q_ref.shape[0]
    for h in range(N_KV_HEADS):
        kh = k_ref[:, h * HEAD_DIM:(h + 1) * HEAD_DIM]
        vh = v_ref[:, 2 * h * HEAD_DIM:(2 * h + 2) * HEAD_DIM]
        q4 = jnp.concatenate(
            [q_ref[:, (Q_PER_KV * h + g) * HEAD_DIM:(Q_PER_KV * h + g + 1) * HEAD_DIM] for g in range(Q_PER_KV)],
            axis=0)
        s = lax.dot_general(q4, kh, (((1,), (1,)), ((), ())), preferred_element_type=F32)
        m = jnp.max(s, axis=-1, keepdims=True)
        p = jnp.exp2(s - m).astype(BF16)
        pv = jnp.dot(p, vh, preferred_element_type=F32)
        o = pv[:, :HEAD_DIM] / pv[:, HEAD_DIM:]
        for g in range(Q_PER_KV):
            hd = Q_PER_KV * h + g
            heads_ref[:, hd * HEAD_DIM:(hd + 1) * HEAD_DIM] = o[g * tq:(g + 1) * tq].astype(BF16)
    x = x_ref[...] + jnp.dot(heads_ref[...], wo_ref[...], preferred_element_type=F32)
    o_ref[:, :D_MODEL] = x
    o_ref[:, D_MODEL:] = _route_rows(x, rg_ref, rw_ref, rb_ref)


def _attn_call(q, k, v, x, w_out, router, B, S):
    T = x.shape[0]
    tq = TQ_ATTN
    nq = S // tq
    row = lambda b, j: (b * nq + j, 0)
    seq = lambda b, j: (b, 0)
    return pl.pallas_call(
        _attn_body,
        grid=(B, nq),
        in_specs=[
            pl.BlockSpec((tq, Q_WIDTH), row),
            pl.BlockSpec((S, KV_WIDTH), seq),
            pl.BlockSpec((S, 2 * KV_WIDTH), seq),
            pl.BlockSpec((tq, D_MODEL), row),
            _const_spec((Q_WIDTH, D_MODEL)),
        ] + [_const_spec(s) for s in _ROUTER_SPECS],
        out_specs=pl.BlockSpec((tq, ROW_WIDTH), row),
        out_shape=jax.ShapeDtypeStruct((T, ROW_WIDTH), F32),
        scratch_shapes=[pltpu.VMEM((tq, Q_WIDTH), BF16)],
        compiler_params=_params("parallel", "parallel"),
        name="attention",
    )(q, k, v, x, w_out, *router)


def _moe_body(ea_ref, eb_ref, src_ref, n_ref, order_ref,
              xr_hbm, g_ref, wga_ref, wgb_ref, wua_ref, wub_ref, wda_ref, wdb_ref,
              out_hbm, xbuf, obuf, gsem, ssem, *, n_tokens):
    del ea_ref, eb_ref
    i = pl.program_id(0)
    tm = xbuf.shape[1]
    slot = i % 2
    other = 1 - slot
    n = n_ref[i]
    n_prev = jnp.where(i > 0, n_ref[jnp.maximum(i - 1, 0)], 0)
    n_prev2 = jnp.where(i > 1, n_ref[jnp.maximum(i - 2, 0)], 0)
    src_prev = src_ref[jnp.maximum(i - 1, 0)]
    src_next = src_ref[jnp.minimum(i + 1, pl.num_programs(0) - 1)]

    def token(pos):
        return order_ref[jnp.minimum(pos, n_tokens - 1)]

    def gather_copy(tok, r, to_slot):
        return pltpu.make_async_copy(xr_hbm.at[pl.ds(tok, 1)], xbuf.at[to_slot, pl.ds(r, 1)], gsem.at[to_slot])

    def scatter_copy(tok, r, from_slot):
        return pltpu.make_async_copy(obuf.at[from_slot, pl.ds(r, 1)], out_hbm.at[pl.ds(tok, 1)], ssem.at[from_slot])

    def wait_gather(in_slot):
        pltpu.make_async_copy(xr_hbm.at[pl.ds(0, tm)], xbuf.at[in_slot], gsem.at[in_slot]).wait()

    def wait_scatter(from_slot, rows):
        aligned = pl.multiple_of((rows // SUBLANES) * SUBLANES, SUBLANES)

        @pl.when(aligned > 0)
        def _():
            pltpu.make_async_copy(obuf.at[from_slot, pl.ds(0, aligned)], out_hbm.at[pl.ds(0, aligned)],
                                  ssem.at[from_slot]).wait()

        def wait_row(r, carry):
            scatter_copy(0, 0, from_slot).wait()
            return carry

        lax.fori_loop(aligned, rows, wait_row, 0)

    @pl.when((i == 0) & (n > 0))
    def _():
        src = src_ref[0]

        def first_gather(r, carry):
            gather_copy(token(src + r), r, 0).start()
            return carry

        lax.fori_loop(0, tm, first_gather, 0)

    @pl.when(n > 0)
    def _():
        wait_gather(slot)
        wait_scatter(slot, n_prev2)

        def issue_rows(group):
            for r in range(tm * group // MOE_ISSUE_GROUPS, tm * (group + 1) // MOE_ISSUE_GROUPS):
                gather_copy(token(src_next + r), r, other).start()

                @pl.when(r < n_prev)
                def _():
                    scatter_copy(token(src_prev + r), r, other).start()

        x = xbuf[slot, :, :D_MODEL]
        meta = xbuf[slot, :, D_MODEL:]
        wa = meta[:, META_WA:META_WA + 1]
        wb = meta[:, META_WB:META_WB + 1]
        rstd = meta[:, META_RSTD:META_RSTD + 1]
        t = (x * rstd * g_ref[...]).astype(BF16)
        gate_a = jnp.dot(t, wga_ref[0], preferred_element_type=F32)
        issue_rows(0)
        up_a = jnp.dot(t, wua_ref[0], preferred_element_type=F32)
        issue_rows(1)
        h_a = (jax.nn.silu(gate_a) * up_a * wa).astype(BF16)
        gate_b = jnp.dot(t, wgb_ref[0], preferred_element_type=F32)
        issue_rows(2)
        up_b = jnp.dot(t, wub_ref[0], preferred_element_type=F32)
        issue_rows(3)
        h_b = (jax.nn.silu(gate_b) * up_b * wb).astype(BF16)
        y = jnp.dot(h_a, wda_ref[0], preferred_element_type=F32)
        issue_rows(4)
        y = y + jnp.dot(h_b, wdb_ref[0], preferred_element_type=F32)
        issue_rows(5)
        obuf[slot] = x + y

    @pl.when((n == 0) & (n_prev > 0))
    def _():
        wait_gather(slot)

        def last_scatter(r, carry):
            scatter_copy(token(src_prev + r), r, other).start()
            return carry

        lax.fori_loop(0, n_prev, last_scatter, 0)
        wait_scatter(other, n_prev)
        wait_scatter(slot, n_prev2)


def _moe_call(xr, norm_g, w_gate, w_up, w_down):
    T = xr.shape[0]
    tm = TM_MOE
    cls = xr[:, D_MODEL + META_CLS].astype(jnp.int32)
    order, ea, eb, src, n = _route_tiles(cls, T, tm)
    nt = ea.shape[0]
    w_in_a = lambda i, ea, eb, src, n, order: (ea[i], 0, 0)
    w_in_b = lambda i, ea, eb, src, n, order: (eb[i], 0, 0)
    const2 = lambda i, *_: (0, 0)
    any_spec = pl.BlockSpec(memory_space=pl.ANY)
    grid_spec = pltpu.PrefetchScalarGridSpec(
        num_scalar_prefetch=5,
        grid=(nt,),
        in_specs=[
            any_spec,
            pl.BlockSpec((1, D_MODEL), const2),
            pl.BlockSpec((1, D_MODEL, EXPERT_FF), w_in_a),
            pl.BlockSpec((1, D_MODEL, EXPERT_FF), w_in_b),
            pl.BlockSpec((1, D_MODEL, EXPERT_FF), w_in_a),
            pl.BlockSpec((1, D_MODEL, EXPERT_FF), w_in_b),
            pl.BlockSpec((1, EXPERT_FF, D_MODEL), w_in_a),
            pl.BlockSpec((1, EXPERT_FF, D_MODEL), w_in_b),
        ],
        out_specs=any_spec,
        scratch_shapes=[
            pltpu.VMEM((2, tm, ROW_WIDTH), F32),
            pltpu.VMEM((2, tm, D_MODEL), F32),
            pltpu.SemaphoreType.DMA((2,)),
            pltpu.SemaphoreType.DMA((2,)),
        ],
    )
    return pl.pallas_call(
        functools.partial(_moe_body, n_tokens=T),
        grid_spec=grid_spec,
        out_shape=jax.ShapeDtypeStruct((T, D_MODEL), F32),
        compiler_params=_params("arbitrary"),
        name="routed_experts",
    )(ea, eb, src, n, order, xr, norm_g, w_gate, w_gate, w_up, w_up, w_down, w_down)


def _sgu_body(x_ref, g_ref, win_ref, vg_ref, ws_ref, bs_ref, wout_ref, rg_ref, rw_ref, rb_ref,
              o_ref, us_ref):
    x = x_ref[...]
    tm = x.shape[0]
    h = (x * _rms_scale(x) * g_ref[...]).astype(BF16)
    z = jnp.dot(h, win_ref[...], preferred_element_type=F32)
    z = 0.5 * z * (1.0 + lax.erf(z * math.sqrt(0.5)))
    u = z[:, :SGU_WIDTH]
    v = z[:, SGU_WIDTH:]
    v = (v * _rms_scale(v) * vg_ref[...]).astype(BF16)
    for c in range(tm // CHUNK):
        rows = slice(c * CHUNK, (c + 1) * CHUNK)
        for g in range(SGU_GROUPS):
            cols = slice(g * SGU_GROUP_DIM, (g + 1) * SGU_GROUP_DIM)
            s = jnp.dot(ws_ref[g], v[rows, cols], preferred_element_type=F32) + bs_ref[:, g:g + 1]
            us_ref[rows, cols] = (u[rows, cols] * s).astype(BF16)
    x = x + jnp.dot(us_ref[...], wout_ref[...], preferred_element_type=F32)
    o_ref[:, :D_MODEL] = x
    o_ref[:, D_MODEL:] = _route_rows(x, rg_ref, rw_ref, rb_ref)


def _sgu_call(x, norm_g, w_in, v_gain, w_s, b_s, w_out, router):
    T = x.shape[0]
    tm = TM_SGU
    row = lambda i: (i, 0)
    return pl.pallas_call(
        _sgu_body,
        grid=(T // tm,),
        in_specs=[
            pl.BlockSpec((tm, D_MODEL), row),
            _const_spec((1, D_MODEL)),
            _const_spec((D_MODEL, 2 * SGU_WIDTH)),
            _const_spec((1, SGU_WIDTH)),
            _const_spec((SGU_GROUPS, CHUNK, CHUNK)),
            _const_spec((CHUNK, SGU_GROUPS)),
            _const_spec((SGU_WIDTH, D_MODEL)),
        ] + [_const_spec(s) for s in _ROUTER_SPECS],
        out_specs=pl.BlockSpec((tm, ROW_WIDTH), row),
        out_shape=jax.ShapeDtypeStruct((T, ROW_WIDTH), F32),
        scratch_shapes=[pltpu.VMEM((tm, SGU_WIDTH), BF16)],
        compiler_params=_params("parallel"),
        name="sgu",
    )(x, norm_g, w_in, v_gain, w_s, b_s, w_out, *router)


def _encoder(x, p):
    B, S, _ = x.shape
    x = x.reshape(B * S, D_MODEL)
    q, k, v = _qkv_call(x, p["attn_norm"], p["attn_w_in"], p["attn_q_gain"], p["attn_k_gain"], S)
    xr = _attn_call(q, k, v, x, p["attn_w_out"], p["router"][0], B, S)
    x = _moe_call(xr, p["moe_norm"][0], p["moe_w_gate"][0], p["moe_w_up"][0], p["moe_w_down"][0])
    xr = _sgu_call(x, p["sgu_norm"], p["sgu_w_in"], p["sgu_v_gain"], p["sgu_w_s"], p["sgu_b_s"], p["sgu_w_out"],
                   p["router"][1])
    x = _moe_call(xr, p["moe_norm"][1], p["moe_w_gate"][1], p["moe_w_up"][1], p["moe_w_down"][1])
    return x.reshape(B, S, D_MODEL)


def kernel(x_prompt, x_sample, attn_norm, attn_w_in, attn_q_gain, attn_k_gain, attn_w_out, sgu_norm, sgu_w_in, sgu_v_gain, sgu_w_s, sgu_b_s, sgu_w_out, moe_norm, moe_w_group, moe_b_group, moe_w_router, moe_b_router, moe_w_gate, moe_w_up, moe_w_down):
    moe_norm = moe_norm.reshape(2, 1, D_MODEL)
    p = {
        "attn_norm": attn_norm[0].reshape(1, D_MODEL),
        "attn_w_in": attn_w_in[0].astype(BF16),
        "attn_q_gain": attn_q_gain[0].reshape(1, HEAD_DIM),
        "attn_k_gain": attn_k_gain[0].reshape(1, HEAD_DIM),
        "attn_w_out": attn_w_out[0].astype(BF16),
        "sgu_norm": sgu_norm[0].reshape(1, D_MODEL),
        "sgu_w_in": sgu_w_in[0].astype(BF16),
        "sgu_v_gain": sgu_v_gain[0].reshape(1, SGU_WIDTH),
        "sgu_w_s": sgu_w_s[0].astype(BF16),
        "sgu_b_s": sgu_b_s[0].T,
        "sgu_w_out": sgu_w_out[0].astype(BF16),
        "moe_norm": moe_norm,
        "router": [_router_operands(moe_norm[l], moe_w_group[l], moe_b_group[l], moe_w_router[l], moe_b_router[l])
                   for l in range(2)],
        "moe_w_gate": moe_w_gate.astype(BF16),
        "moe_w_up": moe_w_up.astype(BF16),
        "moe_w_down": moe_w_down.astype(BF16),
    }
    return _encoder(x_prompt, p), _encoder(x_sample, p)
```
